```python
import jax, jax.numpy as jnp
from jax import lax
import numpy as np

D_MODEL = 1024
BATCH = 2
SEQ = 16384
DEPTH = 2

GRID_W = 64
CTX_LEN = 256
EPS = 1e-6

N_HEADS = 8
N_KV_HEADS = 2
HEAD_DIM = 64
GQA_GROUP = N_HEADS // N_KV_HEADS
WINDOW = 128
BLOCK = 128
ROPE_BASE = 10000.0
Q_DIM = N_HEADS * HEAD_DIM
KV_DIM = N_KV_HEADS * HEAD_DIM

FNET_GROUPS = 4
FNET_GROUP_DIM = 128
FNET_WIDTH = FNET_GROUPS * FNET_GROUP_DIM

CONV_DIM = 512
CONV_K = 3

N_BRANCH = 3

Q_OFF = 0
K_OFF = Q_OFF + Q_DIM
V_OFF = K_OFF + KV_DIM
F_OFF = V_OFF + KV_DIM
CX_OFF = F_OFF + FNET_WIDTH
CB_OFF = CX_OFF + CONV_DIM
CC_OFF = CB_OFF + CONV_DIM
GATE_OFF = CC_OFF + CONV_DIM
IN_DIM = GATE_OFF + N_BRANCH * D_MODEL

D_FF = 2816
N_EXPERTS = 8
TOP_K = 2
D_EXPERT = 3584
N_DENSE = (DEPTH + 1) // 2
N_MOE = DEPTH // 2

kernel_name = "hybrid_gated_parallel_dit_block"


def rms_norm(x, g):
    xf = x.astype(jnp.float32)
    y = xf * lax.rsqrt(jnp.mean(xf * xf, axis=-1, keepdims=True) + EPS)
    return (y * g.astype(jnp.float32)).astype(x.dtype)


def axial_rope_tables(n_tokens):
    rows = n_tokens // GRID_W
    row = jnp.repeat(jnp.arange(rows), GRID_W).astype(jnp.float32)
    col = jnp.tile(jnp.arange(GRID_W), rows).astype(jnp.float32)
    half = HEAD_DIM // 2
    inv_freq = 1.0 / (ROPE_BASE ** (jnp.arange(0, half, 2, dtype=jnp.float32) / half))
    ang = jnp.concatenate([row[:, None] * inv_freq, col[:, None] * inv_freq], axis=-1)
    return jnp.cos(ang), jnp.sin(ang)


def apply_rope(x, cos, sin):
    xf = x.astype(jnp.float32).reshape(x.shape[:-1] + (HEAD_DIM // 2, 2))
    x0, x1 = xf[..., 0], xf[..., 1]
    cs = cos[None, :, None, :]
    sn = sin[None, :, None, :]
    out = jnp.stack([x0 * cs - x1 * sn, x0 * sn + x1 * cs], axis=-1).reshape(x.shape)
    return out.astype(x.dtype)


def latent_window_attention(q, k, v, kc, vc, sink):
    B, S = q.shape[0], q.shape[1]
    nb = S // BLOCK
    scale = HEAD_DIM ** -0.5
    qb = q.reshape(B, nb, BLOCK, N_KV_HEADS, GQA_GROUP, HEAD_DIM).transpose(1, 0, 2, 3, 4, 5)
    pad = ((0, 0), (BLOCK, BLOCK), (0, 0), (0, 0))
    kp = jnp.pad(k, pad)
    vp = jnp.pad(v, pad)
    sink_l = sink.astype(jnp.float32).reshape(1, N_KV_HEADS, GQA_GROUP, 1, 1)
    rel = jnp.arange(3 * BLOCK)[None, :] - BLOCK - jnp.arange(BLOCK)[:, None]
    band = jnp.abs(rel) <= WINDOW

    def block_fn(args):
        i, qi = args
        start = i * BLOCK
        ki = lax.dynamic_slice_in_dim(kp, start, 3 * BLOCK, axis=1)
        vi = lax.dynamic_slice_in_dim(vp, start, 3 * BLOCK, axis=1)
        kpos = start - BLOCK + jnp.arange(3 * BLOCK)
        mask = band & ((kpos >= 0) & (kpos < S))[None, :]
        s_loc = jnp.einsum('bqhgd,bkhd->bhgqk', qi, ki, preferred_element_type=jnp.float32) * scale
        s_loc = jnp.where(mask, s_loc, -jnp.inf)
        s_ctx = jnp.einsum('bqhgd,bkhd->bhgqk', qi, kc, preferred_element_type=jnp.float32) * scale
        s_sink = jnp.broadcast_to(sink_l, s_ctx.shape[:-1] + (1,))
        p = jax.nn.softmax(jnp.concatenate([s_loc, s_ctx, s_sink], axis=-1), axis=-1)
        p_loc = p[..., :3 * BLOCK].astype(v.dtype)
        p_ctx = p[..., 3 * BLOCK:3 * BLOCK + kc.shape[1]].astype(vc.dtype)
        return (jnp.einsum('bhgqk,bkhd->bqhgd', p_loc, vi)
                + jnp.einsum('bhgqk,bkhd->bqhgd', p_ctx, vc))

    out = lax.map(block_fn, (jnp.arange(nb), qb))
    return out.transpose(1, 0, 2, 3, 4, 5).reshape(B, S, Q_DIM)


def context_attention(qc, kc, vc, sink):
    B, L = qc.shape[0], qc.shape[1]
    scale = HEAD_DIM ** -0.5
    qg = qc.reshape(B, L, N_KV_HEADS, GQA_GROUP, HEAD_DIM)
    s = jnp.einsum('bqhgd,bkhd->bhgqk', qg, kc, preferred_element_type=jnp.float32) * scale
    s_sink = jnp.broadcast_to(sink.astype(jnp.float32).reshape(1, N_KV_HEADS, GQA_GROUP, 1, 1), s.shape[:-1] + (1,))
    p = jax.nn.softmax(jnp.concatenate([s, s_sink], axis=-1), axis=-1)
    o = jnp.einsum('bhgqk,bkhd->bqhgd', p[..., :L].astype(vc.dtype), vc)
    return o.reshape(B, L, Q_DIM)


def fourier_mix(u):
    B, N = u.shape[0], u.shape[1]
    ug = u.astype(jnp.float32).reshape(B, N, FNET_GROUPS, FNET_GROUP_DIM)
    f = jnp.fft.fft2(ug, axes=(1, 3), norm="ortho").real
    return f.reshape(B, N, FNET_WIDTH).astype(u.dtype)


def short_conv_mix(xin, bg, cg, w_conv):
    u = cg * xin
    up = jnp.pad(u, ((0, 0), (1, 1), (0, 0)))
    y = up[:, :-2] * w_conv[0] + up[:, 1:-1] * w_conv[1] + up[:, 2:] * w_conv[2]
    return bg * y


def merge_branches(z, attn, w_conv, w_attn_o, w_fnet, w_conv_out, w_o):
    y_attn = attn @ w_attn_o
    y_fnet = fourier_mix(z[..., F_OFF:F_OFF + FNET_WIDTH]) @ w_fnet
    y_conv = short_conv_mix(z[..., CX_OFF:CX_OFF + CONV_DIM], z[..., CB_OFF:CB_OFF + CONV_DIM],
                            z[..., CC_OFF:CC_OFF + CONV_DIM], w_conv) @ w_conv_out
    gates = jax.nn.sigmoid(z[..., GATE_OFF:].astype(jnp.float32)).astype(z.dtype)
    gates = gates.reshape(z.shape[:-1] + (N_BRANCH, D_MODEL))
    merged = gates[..., 0, :] * y_attn + gates[..., 1, :] * y_fnet + gates[..., 2, :] * y_conv
    return merged @ w_o


def swiglu(h, w_gate, w_up, w_down):
    return (jax.nn.silu(h @ w_gate) * (h @ w_up)) @ w_down


def moe_swiglu(h, w_router, b_router, w_gate, w_up, w_down):
    logits = jnp.einsum('bsd,de->bse', h, w_router, preferred_element_type=jnp.float32) + b_router.astype(jnp.float32)
    top_val, top_idx = lax.top_k(logits, TOP_K)
    top_w = jax.nn.softmax(top_val, axis=-1)
    gates = jnp.sum(jax.nn.one_hot(top_idx, N_EXPERTS, dtype=jnp.float32) * top_w[..., None], axis=-2)
    gates = gates.astype(h.dtype)
    out = gates[..., 0:1] * swiglu(h, w_gate[0], w_up[0], w_down[0])
    for e in range(1, N_EXPERTS):
        out = out + gates[..., e:e + 1] * swiglu(h, w_gate[e], w_up[e], w_down[e])
    return out


def setup_inputs(seed: int = 0) -> dict:
    key = jax.random.key(seed)
    ks = jax.random.split(key, 32)
    D = D_MODEL
    nrm = lambda k, shape, s: jax.random.normal(k, shape, jnp.float32) * s
    gain = lambda k: 1.0 + nrm(k, (DEPTH, D), 0.05)
    return {
        "x": nrm(ks[0], (BATCH, SEQ, D), 1.0),
        "c": nrm(ks[1], (BATCH, D), 1.0),
        "ctx": nrm(ks[2], (BATCH, CTX_LEN, D), 1.0),
        "c_ctx": nrm(ks[3], (D,), 1.0),
        "w_mod": nrm(ks[4], (DEPTH, D, 6 * D), 0.5 * D ** -0.5),
        "b_mod": nrm(ks[5], (DEPTH, 6 * D), 0.02),
        "g_pre_mix": gain(ks[6]),
        "g_post_mix": gain(ks[7]),
        "g_pre_ffn": gain(ks[8]),
        "g_post_ffn": gain(ks[9]),
        "w_in": nrm(ks[10], (DEPTH, D, IN_DIM), D ** -0.5),
        "attn_sink": nrm(ks[11], (DEPTH, N_HEADS), 0.5),
        "w_conv": nrm(ks[12], (DEPTH, CONV_K, CONV_DIM), CONV_K ** -0.5),
        "w_attn_o": nrm(ks[13], (DEPTH, Q_DIM, D), Q_DIM ** -0.5),
        "w_fnet": nrm(ks[14], (DEPTH, FNET_WIDTH, D), FNET_WIDTH ** -0.5),
        "w_conv_out": nrm(ks[15], (DEPTH, CONV_DIM, D), CONV_DIM ** -0.5),
        "w_o": nrm(ks[16], (DEPTH, D, D), D ** -0.5),
        "w_ff_gate": nrm(ks[17], (N_DENSE, D, D_FF), D ** -0.5),
        "w_ff_up": nrm(ks[18], (N_DENSE, D, D_FF), D ** -0.5),
        "w_ff_down": nrm(ks[19], (N_DENSE, D_FF, D), D_FF ** -0.5),
        "w_router": nrm(ks[20], (N_MOE, D, N_EXPERTS), D ** -0.5),
        "b_router": nrm(ks[21], (N_MOE, N_EXPERTS), 0.01),
        "w_exp_gate": nrm(ks[22], (N_MOE, N_EXPERTS, D, D_EXPERT), D ** -0.5),
        "w_exp_up": nrm(ks[23], (N_MOE, N_EXPERTS, D, D_EXPERT), D ** -0.5),
        "w_exp_down": nrm(ks[24], (N_MOE, N_EXPERTS, D_EXPERT, D), D_EXPERT ** -0.5),
    }


def reference(x, c, ctx, c_ctx, w_mod, b_mod, g_pre_mix, g_post_mix, g_pre_ffn, g_post_ffn,
              w_in, attn_sink, w_conv, w_attn_o, w_fnet, w_conv_out, w_o,
              w_ff_gate, w_ff_up, w_ff_down, w_router, b_router, w_exp_gate, w_exp_up, w_exp_down):
    B, S = x.shape[0], x.shape[1]
    L = ctx.shape[1]
    D = D_MODEL
    cos, sin = axial_rope_tables(S)
    hctx = ctx
    for l in range(DEPTH):
        last = l == DEPTH - 1
        mod = jax.nn.silu(c) @ w_mod[l] + b_mod[l]
        sh1, sc1, ga1, sh2, sc2, ga2 = [m[:, None, :] for m in jnp.split(mod, 6, axis=-1)]
        if last:
            cmod = jax.nn.silu(c_ctx) @ w_mod[l][:, :2 * D] + b_mod[l][:2 * D]
            csh1, csc1 = jnp.split(cmod, 2, axis=-1)
        else:
            cmod = jax.nn.silu(c_ctx) @ w_mod[l] + b_mod[l]
            csh1, csc1, cga1, csh2, csc2, cga2 = jnp.split(cmod, 6, axis=-1)

        hc = rms_norm(hctx, g_pre_mix[l]) * (1 + csc1) + csh1
        if last:
            zc_kv = hc @ w_in[l][:, K_OFF:F_OFF]
        else:
            zc = hc @ w_in[l]
            zc_kv = zc[..., K_OFF:F_OFF]
        kc = zc_kv[..., :KV_DIM].reshape(B, L, N_KV_HEADS, HEAD_DIM)
        vc = zc_kv[..., KV_DIM:].reshape(B, L, N_KV_HEADS, HEAD_DIM)

        h = rms_norm(x, g_pre_mix[l]) * (1 + sc1) + sh1
        z = h @ w_in[l]
        q = apply_rope(z[..., Q_OFF:K_OFF].reshape(B, S, N_HEADS, HEAD_DIM), cos, sin)
        k = apply_rope(z[..., K_OFF:V_OFF].reshape(B, S, N_KV_HEADS, HEAD_DIM), cos, sin)
        v = z[..., V_OFF:F_OFF].reshape(B, S, N_KV_HEADS, HEAD_DIM)
        attn = latent_window_attention(q, k, v, kc, vc, attn_sink[l])
        mix = merge_branches(z, attn, w_conv[l], w_attn_o[l], w_fnet[l], w_conv_out[l], w_o[l])
        x = x + ga1 * rms_norm(mix, g_post_mix[l])

        h2 = rms_norm(x, g_pre_ffn[l]) * (1 + sc2) + sh2
        if l % 2 == 0:
            f = swiglu(h2, w_ff_gate[l // 2], w_ff_up[l // 2], w_ff_down[l // 2])
        else:
            f = moe_swiglu(h2, w_router[l // 2], b_router[l // 2], w_exp_gate[l // 2],
                           w_exp_up[l // 2], w_exp_down[l // 2])
        x = x + ga2 * rms_norm(f, g_post_ffn[l])

        if not last:
            qc = zc[..., Q_OFF:K_OFF].reshape(B, L, N_HEADS, HEAD_DIM)
            attn_c = context_attention(qc, kc, vc, attn_sink[l])
            mix_c = merge_branches(zc, attn_c, w_conv[l], w_attn_o[l], w_fnet[l], w_conv_out[l], w_o[l])
            hctx = hctx + cga1 * rms_norm(mix_c, g_post_mix[l])
            hc2 = rms_norm(hctx, g_pre_ffn[l]) * (1 + csc2) + csh2
            if l % 2 == 0:
                fc = swiglu(hc2, w_ff_gate[l // 2], w_ff_up[l // 2], w_ff_down[l // 2])
            else:
                fc = moe_swiglu(hc2, w_router[l // 2], b_router[l // 2], w_exp_gate[l // 2],
                                w_exp_up[l // 2], w_exp_down[l // 2])
            hctx = hctx + cga2 * rms_norm(fc, g_post_ffn[l])
    return x
```

```python
import functools
import math

import jax
import jax.numpy as jnp
import numpy as np
from jax import lax
from jax.experimental import pallas as pl
from jax.experimental.pallas import tpu as pltpu

F32 = jnp.float32
BF16 = jnp.bfloat16

D_MODEL = 1024
GRID_W = 64
EPS = 1e-6
N_HEADS = 8
N_KV_HEADS = 2
HEAD_DIM = 64
GQA_GROUP = N_HEADS // N_KV_HEADS
WINDOW = 128
ROPE_BASE = 10000.0
Q_DIM = N_HEADS * HEAD_DIM
KV_DIM = N_KV_HEADS * HEAD_DIM
FNET_GROUPS = 4
FNET_GROUP_DIM = 128
FNET_WIDTH = FNET_GROUPS * FNET_GROUP_DIM
CONV_DIM = 512
N_BRANCH = 3
Q_OFF = 0
K_OFF = Q_OFF + Q_DIM
V_OFF = K_OFF + KV_DIM
F_OFF = V_OFF + KV_DIM
CX_OFF = F_OFF + FNET_WIDTH
CB_OFF = CX_OFF + CONV_DIM
CC_OFF = CB_OFF + CONV_DIM
GATE_OFF = CC_OFF + CONV_DIM
N_EXPERTS = 8

LANES = 128
BF16_SUBLANES = 16
VMEM_LIMIT_BYTES = 56 * 1024 * 1024

KK_DIM = 2 * KV_DIM
P_Q = 0
P_KK = P_Q + Q_DIM
P_VV = P_KK + KK_DIM
P_F = P_VV + KK_DIM
P_CX = P_F + FNET_WIDTH
P_CB = P_CX + CONV_DIM
P_CC = P_CB + CONV_DIM
P_GATE = P_CC + CONV_DIM
P_DIM = P_GATE + N_BRANCH * D_MODEL

NEG_BIG = -1e30


def _cparams(*sem):
    return pltpu.CompilerParams(dimension_semantics=sem, vmem_limit_bytes=VMEM_LIMIT_BYTES)


def _dot(a, b):
    return jnp.dot(a, b, preferred_element_type=F32)


def _dot_nt(a, b):
    return lax.dot_general(a, b, (((1,), (1,)), ((), ())), preferred_element_type=F32)


def _rms(x, g):
    ms = jnp.mean(x * x, axis=-1, keepdims=True)
    return x * lax.rsqrt(ms + EPS) * g


def _const_spec(shape):
    nd = len(shape)
    return pl.BlockSpec(shape, lambda *_: (0,) * nd, pipeline_mode=pl.Buffered(1))


def _mod_kernel(c_ref, w_ref, b_ref, o_ref):
    c = c_ref[...]
    s = c * jax.nn.sigmoid(c)
    s_hi = s.astype(BF16)
    s_lo = (s - s_hi.astype(F32)).astype(BF16)
    w = w_ref[0]
    w_hi = w.astype(BF16)
    w_lo = (w - w_hi.astype(F32)).astype(BF16)
    o_ref[0] = _dot(s_hi, w_hi) + _dot(s_lo, w_hi) + _dot(s_hi, w_lo) + b_ref[0]


def _modulation(c_rows, w_mod, b_mod):
    depth, d, n = w_mod.shape
    tn = 1024
    return pl.pallas_call(
        _mod_kernel,
        grid=(depth, n // tn),
        in_specs=[
            pl.BlockSpec((8, d), lambda l, j: (0, 0)),
            pl.BlockSpec((1, d, tn), lambda l, j: (l, 0, j)),
            pl.BlockSpec((1, 1, tn), lambda l, j: (l, 0, j)),
        ],
        out_specs=pl.BlockSpec((1, 8, tn), lambda l, j: (l, 0, j)),
        out_shape=jax.ShapeDtypeStruct((depth, 8, n), F32),
        compiler_params=_cparams("arbitrary", "arbitrary"),
        name="modulation",
    )(c_rows, w_mod, b_mod)


def _in_proj_kernel(*refs, rope):
    if rope:
        (x_ref, sc_ref, sh_ref, g_ref, w_ref, cos_ref, sin_ref,
         q_ref, kk_ref, vv_ref, f_ref, u_ref, cb_ref, gt_ref) = refs
    else:
        (x_ref, sc_ref, sh_ref, g_ref, w_ref,
         q_ref, kk_ref, vv_ref, f_ref, u_ref, cb_ref, gt_ref) = refs
    x = x_ref[0]
    h = _rms(x, g_ref[...]) * (1.0 + sc_ref[0]) + sh_ref[0]
    hb = h.astype(BF16)

    def proj(c0, width):
        return _dot(hb, w_ref[:, c0:c0 + width])

    if rope:
        cos = cos_ref[...]
        sin = sin_ref[...]
        lane = lax.broadcasted_iota(jnp.int32, cos.shape, 1)
        first_half = (lane % HEAD_DIM) < (HEAD_DIM // 2)

    def rot(zc):
        if not rope:
            return zc
        partner = jnp.where(first_half, pltpu.roll(zc, LANES - HEAD_DIM // 2, 1),
                            pltpu.roll(zc, HEAD_DIM // 2, 1))
        return zc * cos + partner * sin

    scale = HEAD_DIM ** -0.5
    zq = proj(P_Q, Q_DIM)
    for j in range(Q_DIM // LANES):
        q_ref[0, :, j * LANES:(j + 1) * LANES] = (rot(zq[:, j * LANES:(j + 1) * LANES]) * scale).astype(BF16)
    zkv = proj(P_KK, 2 * KK_DIM)
    for j in range(KK_DIM // LANES):
        kk_ref[0, :, j * LANES:(j + 1) * LANES] = rot(zkv[:, j * LANES:(j + 1) * LANES]).astype(BF16)
    vv_ref[0] = zkv[:, KK_DIM:].astype(BF16)
    f_ref[0] = proj(P_F, FNET_WIDTH).astype(BF16)
    u_ref[0] = (proj(P_CC, CONV_DIM) * proj(P_CX, CONV_DIM)).astype(BF16)
    cb_ref[0] = proj(P_CB, CONV_DIM).astype(BF16)
    for j in range(N_BRANCH * D_MODEL // 512):
        gt_ref[0, :, j * 512:(j + 1) * 512] = jax.nn.sigmoid(proj(P_GATE + j * 512, 512)).astype(BF16)


def _in_proj(x, sc, sh, g, w, cos, sin, *, tm):
    bsz, n, d = x.shape
    rope = cos is not None
    row = lambda width: pl.BlockSpec((1, tm, width), lambda b, i: (b, i, 0))
    vec = pl.BlockSpec((1, 1, d), lambda b, i: (b, 0, 0))
    in_specs = [row(d), vec, vec, _const_spec((1, d)), _const_spec((d, P_DIM))]
    args = [x, sc, sh, g, w]
    if rope:
        tab = pl.BlockSpec((tm, LANES), lambda b, i: (i, 0))
        in_specs += [tab, tab]
        args += [cos, sin]
    widths = [Q_DIM, KK_DIM, KK_DIM, FNET_WIDTH, CONV_DIM, CONV_DIM, N_BRANCH * D_MODEL]
    return pl.pallas_call(
        functools.partial(_in_proj_kernel, rope=rope),
        grid=(bsz, n // tm),
        in_specs=in_specs,
        out_specs=[row(wd) for wd in widths],
        out_shape=[jax.ShapeDtypeStruct((bsz, n, wd), BF16) for wd in widths],
        compiler_params=_cparams("parallel", "parallel"),
        name="in_proj_rope" if rope else "in_proj_ctx",
    )(*args)


def _attend(q, keys, vals, sink_ref, mask):
    tq = q.shape[0]
    lane = lax.broadcasted_iota(jnp.int32, (tq, LANES), 1)
    low_half = lane < HEAD_DIM
    outs = []
    for pair in range(N_HEADS // 2):
        g = (2 * pair) // GQA_GROUP
        qc = q[:, pair * LANES:(pair + 1) * LANES]
        halves = []
        for a in range(2):
            head = 2 * pair + a
            qm = jnp.where(low_half if a == 0 else jnp.logical_not(low_half), qc, jnp.zeros_like(qc))
            s = _dot_nt(qm, keys[g])
            if mask is not None:
                s = jnp.where(mask, s, NEG_BIG)
            sink = sink_ref[head]
            m = jnp.maximum(jnp.max(s, axis=-1, keepdims=True), sink)
            p = jnp.exp(s - m)
            denom = jnp.sum(p, axis=-1, keepdims=True) + jnp.exp(sink - m)
            o = _dot(p.astype(BF16), vals[g])
            halves.append(o / denom)
        outs.append(jnp.where(low_half, halves[0], halves[1]))
    return outs


def _attn_kernel(sink_ref, q_ref, kp_ref, kc_ref, kn_ref, vp_ref, vc_ref, vn_ref, kx_ref, vx_ref, o_ref, *, nb):
    i = pl.program_id(1)
    blk = q_ref.shape[1]
    nctx = kx_ref.shape[1]
    keys = [jnp.concatenate([r[0, :, g * LANES:(g + 1) * LANES] for r in (kp_ref, kc_ref, kn_ref, kx_ref)], axis=0)
            for g in range(N_KV_HEADS)]
    vals = [jnp.concatenate([r[0, :, g * LANES:(g + 1) * LANES] for r in (vp_ref, vc_ref, vn_ref, vx_ref)], axis=0)
            for g in range(N_KV_HEADS)]
    nk = 3 * blk + nctx
    qi = lax.broadcasted_iota(jnp.int32, (blk, nk), 0)
    kj = lax.broadcasted_iota(jnp.int32, (blk, nk), 1)
    rel = kj - blk - qi
    band = (rel >= -WINDOW) & (rel <= WINDOW)
    k_min = jnp.where(i > 0, 0, blk)
    k_max = jnp.where(i < nb - 1, 3 * blk, 2 * blk)
    mask = (kj >= 3 * blk) | (band & (kj >= k_min) & (kj < k_max))
    outs = _attend(q_ref[0], keys, vals, sink_ref, mask)
    for pair, o in enumerate(outs):
        o_ref[0, :, pair * LANES:(pair + 1) * LANES] = o.astype(BF16)


def _window_attention(q, kk, vv, kkc, vvc, sink, *, blk):
    bsz, n, _ = q.shape
    nb = n // blk
    nctx = kkc.shape[1]
    qs = pl.BlockSpec((1, blk, Q_DIM), lambda b, i: (b, i, 0))
    prev = pl.BlockSpec((1, blk, KK_DIM), lambda b, i: (b, jnp.maximum(i - 1, 0), 0))
    cur = pl.BlockSpec((1, blk, KK_DIM), lambda b, i: (b, i, 0))
    nxt = pl.BlockSpec((1, blk, KK_DIM), lambda b, i: (b, jnp.minimum(i + 1, nb - 1), 0))
    ctx = pl.BlockSpec((1, nctx, KK_DIM), lambda b, i: (b, 0, 0))
    return pl.pallas_call(
        functools.partial(_attn_kernel, nb=nb),
        grid=(bsz, nb),
        in_specs=[pl.BlockSpec(memory_space=pltpu.SMEM), qs, prev, cur, nxt, prev, cur, nxt, ctx, ctx],
        out_specs=qs,
        out_shape=jax.ShapeDtypeStruct((bsz, n, Q_DIM), BF16),
        compiler_params=_cparams("parallel", "parallel"),
        name="window_attention",
    )(sink, q, kk, kk, kk, vv, vv, vv, kkc, vvc)


def _ctx_attn_kernel(sink_ref, q_ref, kx_ref, vx_ref, o_ref):
    keys = [kx_ref[0, :, g * LANES:(g + 1) * LANES] for g in range(N_KV_HEADS)]
    vals = [vx_ref[0, :, g * LANES:(g + 1) * LANES] for g in range(N_KV_HEADS)]
    outs = _attend(q_ref[0], keys, vals, sink_ref, None)
    for pair, o in enumerate(outs):
        o_ref[0, :, pair * LANES:(pair + 1) * LANES] = o.astype(BF16)


def _context_attention(qc, kkc, vvc, sink):
    bsz, nctx, _ = qc.shape
    qs = pl.BlockSpec((1, nctx, Q_DIM), lambda b: (b, 0, 0))
    ks = pl.BlockSpec((1, nctx, KK_DIM), lambda b: (b, 0, 0))
    return pl.pallas_call(
        _ctx_attn_kernel,
        grid=(bsz,),
        in_specs=[pl.BlockSpec(memory_space=pltpu.SMEM), qs, ks, ks],
        out_specs=qs,
        out_shape=jax.ShapeDtypeStruct((bsz, nctx, Q_DIM), BF16),
        compiler_params=_cparams("parallel"),
        name="context_attention",
    )(sink, qc, kkc, vvc)


def _split_len(n):
    n1 = 1 << ((n.bit_length() - 1 + 1) // 2)
    assert n % n1 == 0 and n == n1 * (n // n1)
    return n1, n // n1


def _dft_tables(n):
    n1, n2 = _split_len(n)
    a = np.arange(n1)
    ang1 = 2.0 * np.pi * np.outer(a, a) / n1
    stage1 = np.concatenate([np.cos(ang1), -np.sin(ang1)], axis=0) / math.sqrt(n1)
    k1 = np.arange(n1)[:, None, None]
    k2 = np.arange(n2)[None, :, None]
    b = np.arange(n2)[None, None, :]
    ang2 = 2.0 * np.pi * ((b * (k1 + n1 * k2)) % n) / n
    gc = np.cos(ang2) / math.sqrt(n2)
    gs = np.sin(ang2) / math.sqrt(n2)
    stage2 = np.concatenate([np.concatenate([gc, gs], axis=2),
                             np.concatenate([-gs, gc], axis=2)], axis=1)
    return jnp.asarray(stage1, BF16), jnp.asarray(stage2, BF16)


def _channel_table():
    c = np.arange(FNET_GROUP_DIM)
    ang = 2.0 * np.pi * np.outer(c, c) / FNET_GROUP_DIM
    t = np.concatenate([np.cos(ang), np.sin(ang)], axis=0) / math.sqrt(FNET_GROUP_DIM)
    return jnp.asarray(t, BF16)


def _channel_mix(v, n_rows, ch_ref, o_ref_slice_store):
    vr = v[:n_rows].astype(BF16)
    vi = v[n_rows:].astype(BF16)
    ch = ch_ref[...]
    for g in range(FNET_GROUPS):
        sl = slice(g * FNET_GROUP_DIM, (g + 1) * FNET_GROUP_DIM)
        lhs = jnp.concatenate([vr[:, sl], vi[:, sl]], axis=1)
        o_ref_slice_store(sl, _dot(lhs, ch).astype(BF16))


def _fft_stage1_kernel(u_ref, t_ref, y_ref):
    n1 = u_ref.shape[1]
    y = _dot(t_ref[...], u_ref[0])
    y_ref[0, 0] = y[:n1].astype(BF16)
    y_ref[0, 1] = y[n1:].astype(BF16)


def _fft_stage2_kernel(y_ref, t_ref, ch_ref, o_ref):
    n2 = y_ref.shape[3]
    y = y_ref[0, :, 0].reshape(2 * n2, FNET_WIDTH)
    v = _dot(t_ref[0], y)

    def store(sl, val):
        o_ref[:, sl] = val

    _channel_mix(v, n2, ch_ref, store)


def _fourier_mix(f, stage1, stage2, chan):
    bsz, n, width = f.shape
    n1, n2 = stage2.shape[0], stage2.shape[1] // 2
    cols = n2 * width
    cw = min(cols, 4096)
    y = pl.pallas_call(
        _fft_stage1_kernel,
        grid=(bsz, cols // cw),
        in_specs=[pl.BlockSpec((1, n1, cw), lambda b, j: (b, 0, j)), _const_spec((2 * n1, n1))],
        out_specs=pl.BlockSpec((1, 2, n1, cw), lambda b, j: (b, 0, 0, j)),
        out_shape=jax.ShapeDtypeStruct((bsz, 2, n1, cols), BF16),
        compiler_params=_cparams("parallel", "parallel"),
        name="fourier_stage1",
    )(f.reshape(bsz, n1, cols), stage1)
    out = pl.pallas_call(
        _fft_stage2_kernel,
        grid=(bsz, n1),
        in_specs=[pl.BlockSpec((1, 2, 1, n2, width), lambda b, k: (b, 0, k, 0, 0)),
                  pl.BlockSpec((1, 2 * n2, 2 * n2), lambda b, k: (k, 0, 0)),
                  _const_spec((2 * FNET_GROUP_DIM, FNET_GROUP_DIM))],
        out_specs=pl.BlockSpec((None, n2, width), lambda b, k: (b, 0, k)),
        out_shape=jax.ShapeDtypeStruct((bsz, n2, n1 * width), BF16),
        compiler_params=_cparams("parallel", "parallel"),
        name="fourier_stage2",
    )(y.reshape(bsz, 2, n1, n2, width), stage2, chan)
    return out.reshape(bsz, n, width)


def _ctx_fourier_kernel(u_ref, t_ref, ch_ref, o_ref):
    n = u_ref.shape[1]
    v = _dot(t_ref[...], u_ref[0])

    def store(sl, val):
        o_ref[0, :, sl] = val

    _channel_mix(v, n, ch_ref, store)


def _ctx_fourier_mix(f, chan):
    bsz, n, width = f.shape
    a = np.arange(n)
    ang = 2.0 * np.pi * (np.outer(a, a) % n) / n
    table = jnp.asarray(np.concatenate([np.cos(ang), -np.sin(ang)], axis=0) / math.sqrt(n), BF16)
    spec = pl.BlockSpec((1, n, width), lambda b: (b, 0, 0))
    return pl.pallas_call(
        _ctx_fourier_kernel,
        grid=(bsz,),
        in_specs=[spec, _const_spec((2 * n, n)), _const_spec((2 * FNET_GROUP_DIM, FNET_GROUP_DIM))],
        out_specs=spec,
        out_shape=jax.ShapeDtypeStruct((bsz, n, width), BF16),
        compiler_params=_cparams("parallel"),
        name="context_fourier",
    )(f, table, chan)


def _merge_kernel(attn_ref, fm_ref, u_ref, up_ref, un_ref, cb_ref, gt_ref, x_ref, ga_ref, gp_ref, w3_ref,
                  wa_ref, wf_ref, wc_ref, wo_ref, o_ref, *, tiles_per_seq):
    i = pl.program_id(1)
    u = u_ref[0].astype(F32)
    tm = u.shape[0]
    prev_row = up_ref[0].astype(F32)[BF16_SUBLANES - 1:BF16_SUBLANES, :]
    next_row = un_ref[0].astype(F32)[0:1, :]
    prev_row = jnp.where(i > 0, prev_row, 0.0)
    next_row = jnp.where(i < tiles_per_seq - 1, next_row, 0.0)
    rows = lax.broadcasted_iota(jnp.int32, u.shape, 0)
    u_up = jnp.where(rows == 0, prev_row, pltpu.roll(u, 1, 0))
    u_dn = jnp.where(rows == tm - 1, next_row, pltpu.roll(u, tm - 1, 0))
    w3 = w3_ref[...]
    y = u_up * w3[0:1, :] + u * w3[1:2, :] + u_dn * w3[2:3, :]
    conv = (cb_ref[0].astype(F32) * y).astype(BF16)
    d = D_MODEL
    merged = gt_ref[0, :, 0:d].astype(F32) * _dot(attn_ref[0], wa_ref[...])
    merged = merged + gt_ref[0, :, d:2 * d].astype(F32) * _dot(fm_ref[0], wf_ref[...])
    merged = merged + gt_ref[0, :, 2 * d:3 * d].astype(F32) * _dot(conv, wc_ref[...])
    mix = _dot(merged.astype(BF16), wo_ref[...])
    o_ref[0] = x_ref[0] + ga_ref[0] * _rms(mix, gp_ref[...])


def _merge(attn, fm, u, cb, gt, x, ga, gpost, w3, wa, wf, wc, wo, *, tm):
    bsz, n, d = x.shape
    tiles = n // tm
    halo = BF16_SUBLANES
    row = lambda width: pl.BlockSpec((1, tm, width), lambda b, i: (b, i, 0))
    prev = pl.BlockSpec((1, halo, CONV_DIM), lambda b, i: (b, jnp.maximum(i * (tm // halo) - 1, 0), 0))
    nxt = pl.BlockSpec((1, halo, CONV_DIM), lambda b, i: (b, jnp.minimum((i + 1) * (tm // halo), n // halo - 1), 0))
    vec = pl.BlockSpec((1, 1, d), lambda b, i: (b, 0, 0))
    return pl.pallas_call(
        functools.partial(_merge_kernel, tiles_per_seq=tiles),
        grid=(bsz, tiles),
        in_specs=[row(Q_DIM), row(FNET_WIDTH), row(CONV_DIM), prev, nxt, row(CONV_DIM), row(N_BRANCH * d), row(d),
                  vec, _const_spec((1, d)), _const_spec((8, CONV_DIM)),
                  _const_spec((Q_DIM, d)), _const_spec((FNET_WIDTH, d)), _const_spec((CONV_DIM, d)),
                  _const_spec((d, d))],
        out_specs=row(d),
        out_shape=jax.ShapeDtypeStruct((bsz, n, d), F32),
        compiler_params=_cparams("parallel", "parallel"),
        name="merge_branches",
    )(attn, fm, u, u, u, cb, gt, x, ga, gpost, w3, wa, wf, wc, wo)


def _ffn_kernel(x_ref, sc_ref, sh_ref, ga_ref, gpre_ref, gpost_ref, wg_ref, wu_ref, wd_ref, o_ref, *, chunk):
    x = x_ref[0]
    hb = (_rms(x, gpre_ref[...]) * (1.0 + sc_ref[0]) + sh_ref[0]).astype(BF16)
    d_ff = wg_ref.shape[1]
    acc = None
    for c0 in range(0, d_ff, chunk):
        a = _dot(hb, wg_ref[:, c0:c0 + chunk])
        b = _dot(hb, wu_ref[:, c0:c0 + chunk])
        t = (a * jax.nn.sigmoid(a) * b).astype(BF16)
        part = _dot(t, wd_ref[c0:c0 + chunk, :])
        acc = part if acc is None else acc + part
    o_ref[0] = x + ga_ref[0] * _rms(acc, gpost_ref[...])


def _dense_ffn(x, sc, sh, ga, gpre, gpost, wg, wu, wd, *, tm):
    bsz, n, d = x.shape
    d_ff = wg.shape[1]
    chunk = d_ff // 2 if (d_ff // 2) % LANES == 0 else d_ff
    row = pl.BlockSpec((1, tm, d), lambda b, i: (b, i, 0))
    vec = pl.BlockSpec((1, 1, d), lambda b, i: (b, 0, 0))
    return pl.pallas_call(
        functools.partial(_ffn_kernel, chunk=chunk),
        grid=(bsz, n // tm),
        in_specs=[row, vec, vec, vec, _const_spec((1, d)), _const_spec((1, d)),
                  _const_spec((d, d_ff)), _const_spec((d, d_ff)), _const_spec((d_ff, d))],
        out_specs=row,
        out_shape=jax.ShapeDtypeStruct((bsz, n, d), F32),
        compiler_params=_cparams("parallel", "parallel"),
        name="dense_swiglu",
    )(x, sc, sh, ga, gpre, gpost, wg, wu, wd)


ROUTE_I1, ROUTE_I2, ROUTE_W1, ROUTE_W2, ROUTE_R1, ROUTE_R2 = range(6)


def _moe_route_kernel(x_ref, sc_ref, sh_ref, gpre_ref, wr_ref, br_ref, h_ref, route_ref, cnt_ref, run_ref):
    first = (pl.program_id(0) == 0) & (pl.program_id(1) == 0)

    @pl.when(first)
    def _():
        run_ref[...] = jnp.zeros_like(run_ref)

    x = x_ref[0]
    tm = x.shape[0]
    h = _rms(x, gpre_ref[...]) * (1.0 + sc_ref[0]) + sh_ref[0]
    h_ref[0] = h
    h_hi = h.astype(BF16)
    h_lo = (h - h_hi.astype(F32)).astype(BF16)
    w = wr_ref[...]
    w_hi = w.astype(BF16)
    w_lo = (w - w_hi.astype(F32)).astype(BF16)
    logits = _dot(h_hi, w_hi) + _dot(h_lo, w_hi) + _dot(h_hi, w_lo) + br_ref[...]
    lane = lax.broadcasted_iota(jnp.int32, (tm, LANES), 1)
    logits = jnp.where(lane < N_EXPERTS, logits, NEG_BIG)
    v1 = jnp.max(logits, axis=-1, keepdims=True)
    i1 = jnp.min(jnp.where(logits == v1, lane, LANES), axis=-1, keepdims=True)
    rest = jnp.where(lane == i1, NEG_BIG, logits)
    v2 = jnp.max(rest, axis=-1, keepdims=True)
    i2 = jnp.min(jnp.where(rest == v2, lane, LANES), axis=-1, keepdims=True)
    e = jnp.exp(v2 - v1)
    w1 = 1.0 / (1.0 + e)
    w2 = e / (1.0 + e)
    chosen = ((lane == i1) | (lane == i2)).astype(F32)
    r = lax.broadcasted_iota(jnp.int32, (tm, tm), 0)
    c = lax.broadcasted_iota(jnp.int32, (tm, tm), 1)
    before = (c < r).astype(BF16)
    rank = _dot(before, chosen.astype(BF16)) + run_ref[0:1, :]
    r1 = jnp.sum(jnp.where(lane == i1, rank, 0.0), axis=-1, keepdims=True)
    r2 = jnp.sum(jnp.where(lane == i2, rank, 0.0), axis=-1, keepdims=True)
    run_ref[...] = run_ref[...] + jnp.sum(chosen, axis=0, keepdims=True)
    cnt_ref[...] = run_ref[...]
    fields = [i1.astype(F32), i2.astype(F32), w1, w2, r1, r2]
    route = jnp.zeros((tm, LANES), F32)
    for idx, val in enumerate(fields):
        route = jnp.where(lane == idx, val, route)
    route_ref[0] = route


def _moe_route(x, sc, sh, gpre, wr, br, *, tm):
    bsz, n, d = x.shape
    row = pl.BlockSpec((1, tm, d), lambda b, i: (b, i, 0))
    vec = pl.BlockSpec((1, 1, d), lambda b, i: (b, 0, 0))
    return pl.pallas_call(
        _moe_route_kernel,
        grid=(bsz, n // tm),
        in_specs=[row, vec, vec, _const_spec((1, d)), _const_spec((d, LANES)), _const_spec((1, LANES))],
        out_specs=[row, pl.BlockSpec((1, tm, LANES), lambda b, i: (b, i, 0)),
                   pl.BlockSpec((8, LANES), lambda b, i: (0, 0))],
        out_shape=[jax.ShapeDtypeStruct((bsz, n, d), F32), jax.ShapeDtypeStruct((bsz, n, LANES), F32),
                   jax.ShapeDtypeStruct((8, LANES), F32)],
        scratch_shapes=[pltpu.VMEM((8, LANES), F32)],
        compiler_params=_cparams("arbitrary", "arbitrary"),
        name="moe_route",
    )(x, sc, sh, gpre, wr, br)


def _dispatch_kernel(pos_ref, h_ref, xs_ref, sem):
    td = h_ref.shape[0]

    def row_copy(r, k):
        return pltpu.make_async_copy(h_ref.at[pl.ds(r, 1)], xs_ref.at[pl.ds(pos_ref[0, 0, 2 * r + k], 1)], sem)

    def issue(r, carry):
        row_copy(r, 0).start()
        row_copy(r, 1).start()
        return carry

    lax.fori_loop(0, td, issue, 0)

    def drain(r, carry):
        row_copy(r, 0).wait()
        row_copy(r, 1).wait()
        return carry

    lax.fori_loop(0, td, drain, 0)


def _dispatch(h, pos, *, td):
    t, d = h.shape
    return pl.pallas_call(
        _dispatch_kernel,
        grid=(t // td,),
        in_specs=[pl.BlockSpec((1, 1, 2 * td), lambda i: (i, 0, 0), memory_space=pltpu.SMEM),
                  pl.BlockSpec((td, d), lambda i: (i, 0))],
        out_specs=pl.BlockSpec(memory_space=pl.ANY),
        out_shape=jax.ShapeDtypeStruct((2 * t, d), F32),
        scratch_shapes=[pltpu.SemaphoreType.DMA],
        compiler_params=pltpu.CompilerParams(dimension_semantics=("arbitrary",), has_side_effects=True,
                                             vmem_limit_bytes=VMEM_LIMIT_BYTES),
        name="moe_dispatch",
    )(pos.reshape(t // td, 1, 2 * td), h)


def _expert_kernel(tile_ref, exp_ref, lo_ref, hi_ref, first_ref, xs_ref, wg_ref, wu_ref, wd_ref, o_ref,
                   xb_ref, acc_ref, *, nj):
    s = pl.program_id(0)
    j = pl.program_id(1)
    lo = lo_ref[s]
    hi = hi_ref[s]

    @pl.when((j == 0) & (first_ref[s] == 1))
    def _():
        o_ref[...] = jnp.zeros_like(o_ref)

    @pl.when(hi > lo)
    def _():
        @pl.when(j == 0)
        def _():
            xb_ref[...] = xs_ref[...].astype(BF16)

        xb = xb_ref[...]
        a = _dot(xb, wg_ref[0])
        b = _dot(xb, wu_ref[0])
        part = _dot((a * jax.nn.sigmoid(a) * b).astype(BF16), wd_ref[0])

        @pl.when(j == 0)
        def _():
            acc_ref[...] = part

        @pl.when(j > 0)
        def _():
            acc_ref[...] = acc_ref[...] + part

        @pl.when(j == nj - 1)
        def _():
            rows = lax.broadcasted_iota(jnp.int32, acc_ref.shape, 0)
            o_ref[...] = jnp.where((rows >= lo) & (rows < hi), acc_ref[...], o_ref[...])


def _expert_ffn(xs, sched, wg, wu, wd, *, tmx, tf):
    slots, d = xs.shape
    d_e = wg.shape[2]
    nj = d_e // tf
    n_steps = sched[0].shape[0]

    def jj(s, j, lo, hi):
        return jnp.where(hi[s] > lo[s], j, nj - 1)

    grid_spec = pltpu.PrefetchScalarGridSpec(
        num_scalar_prefetch=5,
        grid=(n_steps, nj),
        in_specs=[
            pl.BlockSpec((tmx, d), lambda s, j, tile, ex, lo, hi, fi: (tile[s], 0)),
            pl.BlockSpec((1, d, tf), lambda s, j, tile, ex, lo, hi, fi: (ex[s], 0, jj(s, j, lo, hi))),
            pl.BlockSpec((1, d, tf), lambda s, j, tile, ex, lo, hi, fi: (ex[s], 0, jj(s, j, lo, hi))),
            pl.BlockSpec((1, tf, d), lambda s, j, tile, ex, lo, hi, fi: (ex[s], jj(s, j, lo, hi), 0)),
        ],
        out_specs=pl.BlockSpec((tmx, d), lambda s, j, tile, ex, lo, hi, fi: (tile[s], 0)),
        scratch_shapes=[pltpu.VMEM((tmx, d), BF16), pltpu.VMEM((tmx, d), F32)],
    )
    return pl.pallas_call(
        functools.partial(_expert_kernel, nj=nj),
        grid_spec=grid_spec,
        out_shape=jax.ShapeDtypeStruct((slots, d), F32),
        compiler_params=_cparams("arbitrary", "arbitrary"),
        name="moe_experts",
    )(*sched, xs, wg, wu, wd)


def _combine_kernel(pos_ref, route_ref, x_ref, ga_ref, gpost_ref, ys_ref, o_ref, buf_ref, sem):
    tc = x_ref.shape[0]

    def row_copy(r, k):
        return pltpu.make_async_copy(ys_ref.at[pl.ds(pos_ref[0, 0, 2 * r + k], 1)], buf_ref.at[k, pl.ds(r, 1)], sem)

    def issue(r, carry):
        row_copy(r, 0).start()
        row_copy(r, 1).start()
        return carry

    lax.fori_loop(0, tc, issue, 0)

    def drain(r, carry):
        row_copy(r, 0).wait()
        row_copy(r, 1).wait()
        return carry

    lax.fori_loop(0, tc, drain, 0)
    route = route_ref[...]
    f = route[:, ROUTE_W1:ROUTE_W1 + 1] * buf_ref[0] + route[:, ROUTE_W2:ROUTE_W2 + 1] * buf_ref[1]
    o_ref[...] = x_ref[...] + ga_ref[0] * _rms(f, gpost_ref[...])


def _combine(ys, pos, route, x, ga, gpost, *, tc, tokens_per_batch):
    t, d = x.shape
    per = tokens_per_batch // tc
    return pl.pallas_call(
        _combine_kernel,
        grid=(t // tc,),
        in_specs=[pl.BlockSpec((1, 1, 2 * tc), lambda i: (i, 0, 0), memory_space=pltpu.SMEM),
                  pl.BlockSpec((tc, LANES), lambda i: (i, 0)),
                  pl.BlockSpec((tc, d), lambda i: (i, 0)),
                  pl.BlockSpec((1, 1, d), lambda i: (i // per, 0, 0)),
                  _const_spec((1, d)),
                  pl.BlockSpec(memory_space=pl.ANY)],
        out_specs=pl.BlockSpec((tc, d), lambda i: (i, 0)),
        out_shape=jax.ShapeDtypeStruct((t, d), F32),
        scratch_shapes=[pltpu.VMEM((2, tc, d), F32), pltpu.SemaphoreType.DMA],
        compiler_params=_cparams("arbitrary"),
        name="moe_combine",
    )(pos.reshape(t // tc, 1, 2 * tc), route, x, ga, gpost, ys)


def _expert_schedule(counts, tmx, n_tiles):
    counts = counts.astype(jnp.int32)
    ends = jnp.cumsum(counts)
    offs = ends - counts
    first_tile = offs // tmx
    last_tile = jnp.maximum(ends - 1, 0) // tmx
    visits = jnp.where(counts > 0, last_tile - first_tile + 1, 0)
    vis_end = jnp.cumsum(visits)
    vis_start = vis_end - visits
    total = vis_end[-1]
    n_steps = n_tiles + N_EXPERTS - 1
    s = jnp.arange(n_steps, dtype=jnp.int32)
    sc = jnp.minimum(s, total - 1)
    ex = jnp.sum((vis_end[None, :] <= sc[:, None]).astype(jnp.int32), axis=1)
    tile = first_tile[ex] + sc - vis_start[ex]
    live = s < total
    lo = jnp.where(live, jnp.maximum(offs[ex], tile * tmx) - tile * tmx, 0)
    hi = jnp.where(live, jnp.minimum(ends[ex], (tile + 1) * tmx) - tile * tmx, 0)
    prev_tile = jnp.concatenate([jnp.full((1,), -1, jnp.int32), tile[:-1]])
    first = (live & (tile != prev_tile)).astype(jnp.int32)
    return offs, (tile.astype(jnp.int32), ex.astype(jnp.int32), lo.astype(jnp.int32), hi.astype(jnp.int32), first)


def _moe_ffn(x, sc, sh, ga, gpre, gpost, wr, br, wg, wu, wd):
    bsz, n, d = x.shape
    t = bsz * n
    h, route, cnt = _moe_route(x, sc, sh, gpre, wr, br, tm=512)
    route = route.reshape(t, LANES)
    tmx = 1024
    n_tiles = 2 * t // tmx
    offs, sched = _expert_schedule(cnt[0, :N_EXPERTS], tmx, n_tiles)
    idx = route[:, ROUTE_I1:ROUTE_I2 + 1].astype(jnp.int32)
    pos = offs[idx] + route[:, ROUTE_R1:ROUTE_R2 + 1].astype(jnp.int32)
    xs = _dispatch(h.reshape(t, d), pos, td=512)
    ys = _expert_ffn(xs, sched, wg, wu, wd, tmx=tmx, tf=512)
    out = _combine(ys, pos, route, x.reshape(t, d), ga, gpost, tc=512, tokens_per_batch=n)
    return out.reshape(bsz, n, d)


def _deinterleave(n_heads):
    base = np.concatenate([np.arange(0, HEAD_DIM, 2), np.arange(1, HEAD_DIM, 2)])
    return np.concatenate([h * HEAD_DIM + base for h in range(n_heads)])


def _pack_w_in(w):
    q = w[:, Q_OFF:K_OFF][:, _deinterleave(N_HEADS)]
    k = w[:, K_OFF:V_OFF][:, _deinterleave(N_KV_HEADS)]
    v = w[:, V_OFF:F_OFF]
    dup = lambda m: jnp.concatenate([m[:, g * HEAD_DIM:(g + 1) * HEAD_DIM] for g in range(N_KV_HEADS) for _ in range(2)],
                                    axis=1)
    return jnp.concatenate([q, dup(k), dup(v), w[:, F_OFF:]], axis=1).astype(BF16)


def _rope_tables(n):
    rows = n // GRID_W
    row = np.repeat(np.arange(rows), GRID_W).astype(np.float32)
    col = np.tile(np.arange(GRID_W), rows).astype(np.float32)
    half = HEAD_DIM // 2
    inv_freq = (1.0 / (ROPE_BASE ** (np.arange(0, half, 2, dtype=np.float32) / half))).astype(np.float32)
    ang = np.concatenate([row[:, None] * inv_freq, col[:, None] * inv_freq], axis=-1)
    cos = np.cos(ang.astype(np.float64))
    sin = np.sin(ang.astype(np.float64))
    cos64 = np.concatenate([cos, cos], axis=1)
    sin64 = np.concatenate([-sin, sin], axis=1)
    reps = LANES // HEAD_DIM
    return jnp.asarray(np.tile(cos64, (1, reps)), F32), jnp.asarray(np.tile(sin64, (1, reps)), F32)


def kernel(x, c, ctx, c_ctx, w_mod, b_mod, g_pre_mix, g_post_mix, g_pre_ffn, g_post_ffn, w_in, attn_sink, w_conv,
           w_attn_o, w_fnet, w_conv_out, w_o, w_ff_gate, w_ff_up, w_ff_down, w_router, b_router, w_exp_gate,
           w_exp_up, w_exp_down):
    bsz, n, d = x.shape
    nctx = ctx.shape[1]
    depth = w_mod.shape[0]
    assert d == D_MODEL and bsz + 1 <= 8

    cos, sin = _rope_tables(n)
    stage1, stage2 = _dft_tables(n)
    chan = _channel_table()

    c_rows = jnp.zeros((8, d), F32).at[:bsz].set(c).at[bsz].set(c_ctx)
    mod = _modulation(c_rows, w_mod, b_mod.reshape(depth, 1, 6 * d))

    hctx = ctx
    for l in range(depth):
        last = l == depth - 1
        parts = [mod[l, :bsz, i * d:(i + 1) * d].reshape(bsz, 1, d) for i in range(6)]
        sh1, sc1, ga1, sh2, sc2, ga2 = parts
        cparts = [jnp.broadcast_to(mod[l, bsz, i * d:(i + 1) * d].reshape(1, 1, d), (bsz, 1, d)) for i in range(6)]
        csh1, csc1, cga1, csh2, csc2, cga2 = cparts

        w_l = _pack_w_in(w_in[l])
        g_pre = g_pre_mix[l].reshape(1, d)
        g_post = g_post_mix[l].reshape(1, d)
        g_pre2 = g_pre_ffn[l].reshape(1, d)
        g_post2 = g_post_ffn[l].reshape(1, d)
        w3 = jnp.zeros((8, CONV_DIM), F32).at[:3].set(w_conv[l])
        wa = w_attn_o[l].astype(BF16)
        wf = w_fnet[l].astype(BF16)
        wc = w_conv_out[l].astype(BF16)
        wo = w_o[l].astype(BF16)
        sink = attn_sink[l].astype(F32)

        qc, kkc, vvc, fc, uc, cbc, gtc = _in_proj(hctx, csc1, csh1, g_pre, w_l, None, None, tm=nctx)

        q, kk, vv, f, u, cb, gt = _in_proj(x, sc1, sh1, g_pre, w_l, cos, sin, tm=512)
        attn = _window_attention(q, kk, vv, kkc, vvc, sink, blk=WINDOW)
        fm = _fourier_mix(f, stage1, stage2, chan)
        x = _merge(attn, fm, u, cb, gt, x, ga1, g_post, w3, wa, wf, wc, wo, tm=512)

        if l % 2 == 0:
            wg = w_ff_gate[l // 2].astype(BF16)
            wu = w_ff_up[l // 2].astype(BF16)
            wd = w_ff_down[l // 2].astype(BF16)
            x = _dense_ffn(x, sc2, sh2, ga2, g_pre2, g_post2, wg, wu, wd, tm=512)
        else:
            wr = jnp.zeros((d, LANES), F32).at[:, :N_EXPERTS].set(w_router[l // 2])
            br = jnp.zeros((1, LANES), F32).at[0, :N_EXPERTS].set(b_router[l // 2])
            wg = w_exp_gate[l // 2].astype(BF16)
            wu = w_exp_up[l // 2].astype(BF16)
            wd = w_exp_down[l // 2].astype(BF16)
            x = _moe_ffn(x, sc2, sh2, ga2, g_pre2, g_post2, wr, br, wg, wu, wd)

        if not last:
            attn_c = _context_attention(qc, kkc, vvc, sink)
            fm_c = _ctx_fourier_mix(fc, chan)
            hctx = _merge(attn_c, fm_c, uc, cbc, gtc, hctx, cga1, g_post, w3, wa, wf, wc, wo, tm=nctx)
            if l % 2 == 0:
                hctx = _dense_ffn(hctx, csc2, csh2, cga2, g_pre2, g_post2, wg, wu, wd, tm=nctx)
            else:
                hctx = _moe_ffn(hctx, csc2, csh2, cga2, g_pre2, g_post2, wr, br, wg, wu, wd)
    return x
```
